```python
import jax, jax.numpy as jnp
from jax import lax
import numpy as np

D_MODEL = 1024
BATCH = 16
SEQ = 4096
DEPTH = 2
DEC_BATCH = 8
DEC_SEQ = 64
PAST_LEN = 2048

CHUNK = 64
N_BRANCH = 4
W_BR = D_MODEL // 2
GMLP_CHUNK = 128
GMLP_HEADS = 4
GMLP_HD = W_BR // GMLP_HEADS
CONV_K = 31
POOL_WINDOWS = (2, 4, 8, 16)
POOL_GROUPS = len(POOL_WINDOWS)
POOL_GD = W_BR // POOL_GROUPS
POOL_STATE = max(POOL_WINDOWS) - 1
SHORT_K = 3
N_PARTS = 12
N_IN = N_PARTS * W_BR + N_BRANCH * D_MODEL
SPLITS = tuple(W_BR * i for i in range(1, N_PARTS + 1))
EPS = 1e-6

kernel_name = 'hybrid_gated_branch_streaming_step'


def rmsnorm(x, g):
    xf = x.astype(jnp.float32)
    r = lax.rsqrt(jnp.mean(xf * xf, axis=-1, keepdims=True) + EPS)
    return (xf * r).astype(x.dtype) * g


def layernorm(x, g, b):
    xf = x.astype(jnp.float32)
    mu = jnp.mean(xf, axis=-1, keepdims=True)
    var = jnp.mean(jnp.square(xf - mu), axis=-1, keepdims=True)
    return ((xf - mu) * lax.rsqrt(var + EPS)).astype(x.dtype) * g + b


def causal_depthwise(ext, w):
    c = ext.shape[-1]
    return lax.conv_general_dilated(ext, w[:, None, :].astype(ext.dtype), window_strides=(1,),
                                    padding='VALID', dimension_numbers=('NWC', 'WIO', 'NWC'),
                                    feature_group_count=c)


def gmlp_spatial(u, v, w_s, b_s):
    bsz, t, _ = v.shape
    ln = min(t, GMLP_CHUNK)
    n = t // ln
    mask = jnp.tril(jnp.ones((ln, ln), v.dtype))
    w = w_s[:, :ln, :ln] * mask
    vc = v.reshape(bsz, n, ln, GMLP_HEADS, GMLP_HD)
    mixed = jnp.einsum('hts,bnshc->bnthc', w, vc) + b_s[:, :ln].T[None, None, :, :, None]
    return u * mixed.reshape(bsz, t, W_BR)


def multiscale_pool(ext, start_pos, pool_w, pool_scale):
    bsz = ext.shape[0]
    t = ext.shape[1] - POOL_STATE
    xf = ext.astype(jnp.float32)
    cs = jnp.concatenate([jnp.zeros_like(xf[:, :1]), jnp.cumsum(xf, axis=1)], axis=1)
    pos = start_pos + jnp.arange(t)
    x_tok = xf[:, POOL_STATE:]
    outs = []
    for g, w in enumerate(POOL_WINDOWS):
        sl = slice(g * POOL_GD, (g + 1) * POOL_GD)
        hi = cs[:, POOL_STATE + 1:, sl]
        lo = cs[:, POOL_STATE + 1 - w:POOL_STATE + 1 - w + t, sl]
        cnt = jnp.minimum(pos + 1, w).astype(jnp.float32)[None, :, None]
        outs.append((hi - lo) / cnt - x_tok[:, :, sl])
    p = jnp.stack(outs, axis=2).astype(ext.dtype)
    p = jnp.einsum('btgc,gcd->btgd', p, pool_w).reshape(bsz, t, W_BR)
    return p * pool_scale


def trunk(x, conv_states, pool_states, short_states, start_pos, g_pre, g_post, w_in, b_gate,
          ln_v_g, ln_v_b, w_s, b_s, conv_b_w, conv_b_bias, ln_b_g, ln_b_b, pool_w, pool_scale,
          conv_d_w, w_branch, w_out):
    bsz, t, _ = x.shape
    new_conv, new_pool, new_short, v_rows = [], [], [], []
    for l in range(DEPTH):
        h = rmsnorm(x, g_pre[l])
        proj = h @ w_in[l]
        (a_u, a_v, a_z, b_a, b_b, b_z, c_x, c_z, d_b, d_c, d_x, d_z,
         gate_logits) = jnp.split(proj, SPLITS, axis=-1)
        v_n = layernorm(a_v, ln_v_g[l], ln_v_b[l])
        out_a = gmlp_spatial(a_u, v_n, w_s[l], b_s[l]) * jax.nn.silu(a_z)
        glu = b_a * jax.nn.sigmoid(b_b)
        ext_b = jnp.concatenate([conv_states[l], glu], axis=1)
        cb = causal_depthwise(ext_b, conv_b_w[l]) + conv_b_bias[l]
        out_b = jax.nn.silu(layernorm(cb, ln_b_g[l], ln_b_b[l])) * jax.nn.silu(b_z)
        ext_c = jnp.concatenate([pool_states[l], c_x], axis=1)
        out_c = multiscale_pool(ext_c, start_pos, pool_w[l], pool_scale[l]) * jax.nn.silu(c_z)
        ext_d = jnp.concatenate([short_states[l], d_c * d_x], axis=1)
        out_d = d_b * causal_depthwise(ext_d, conv_d_w[l]) * jax.nn.silu(d_z)
        gates = jax.nn.sigmoid(gate_logits + b_gate[l]).reshape(bsz, t, N_BRANCH, D_MODEL)
        merged = gates[:, :, 0] * (out_a @ w_branch[l, 0])
        merged = merged + gates[:, :, 1] * (out_b @ w_branch[l, 1])
        merged = merged + gates[:, :, 2] * (out_c @ w_branch[l, 2])
        merged = merged + gates[:, :, 3] * (out_d @ w_branch[l, 3])
        y = merged @ w_out[l]
        x = x + rmsnorm(y, g_post[l])
        new_conv.append(ext_b[:, -(CONV_K - 1):])
        new_pool.append(ext_c[:, -POOL_STATE:])
        new_short.append(ext_d[:, -(SHORT_K - 1):])
        v_rows.append(v_n)
    return x, jnp.stack(new_conv), jnp.stack(new_pool), jnp.stack(new_short), jnp.stack(v_rows)


def setup_inputs(seed: int = 0) -> dict:
    key = jax.random.key(seed)
    ks = jax.random.split(key, 22)
    n = jax.random.normal
    f32 = jnp.float32
    return {
        'x_prompt': n(ks[0], (BATCH, SEQ, D_MODEL), f32),
        'x_sample': n(ks[1], (DEC_BATCH, DEC_SEQ, D_MODEL), f32),
        'state_conformer_conv': 0.5 * n(ks[2], (DEPTH, DEC_BATCH, CONV_K - 1, W_BR), f32),
        'state_pool': n(ks[3], (DEPTH, DEC_BATCH, POOL_STATE, W_BR), f32),
        'state_short_conv': 0.5 * n(ks[4], (DEPTH, DEC_BATCH, SHORT_K - 1, W_BR), f32),
        'g_pre': 1.0 + 0.05 * n(ks[5], (DEPTH, D_MODEL), f32),
        'g_post': 1.0 + 0.05 * n(ks[6], (DEPTH, D_MODEL), f32),
        'w_in': n(ks[7], (DEPTH, D_MODEL, N_IN), f32) * D_MODEL ** -0.5,
        'b_gate': 0.02 * n(ks[8], (DEPTH, N_BRANCH * D_MODEL), f32),
        'ln_v_g': 1.0 + 0.05 * n(ks[9], (DEPTH, W_BR), f32),
        'ln_v_b': 0.02 * n(ks[10], (DEPTH, W_BR), f32),
        'w_s': n(ks[11], (DEPTH, GMLP_HEADS, GMLP_CHUNK, GMLP_CHUNK), f32) * GMLP_CHUNK ** -0.5,
        'b_s': 1.0 + 0.1 * n(ks[12], (DEPTH, GMLP_HEADS, GMLP_CHUNK), f32),
        'conv_b_w': n(ks[13], (DEPTH, CONV_K, W_BR), f32) * CONV_K ** -0.5,
        'conv_b_bias': 0.02 * n(ks[14], (DEPTH, W_BR), f32),
        'ln_b_g': 1.0 + 0.05 * n(ks[15], (DEPTH, W_BR), f32),
        'ln_b_b': 0.02 * n(ks[16], (DEPTH, W_BR), f32),
        'pool_w': n(ks[17], (DEPTH, POOL_GROUPS, POOL_GD, POOL_GD), f32) * POOL_GD ** -0.5,
        'pool_scale': 1.0 + 0.1 * n(ks[18], (DEPTH, W_BR), f32),
        'conv_d_w': n(ks[19], (DEPTH, SHORT_K, W_BR), f32) * SHORT_K ** -0.5,
        'w_branch': n(ks[20], (DEPTH, N_BRANCH, W_BR, D_MODEL), f32) * W_BR ** -0.5,
        'w_out': n(ks[21], (DEPTH, D_MODEL, D_MODEL), f32) * D_MODEL ** -0.5,
    }


def reference(x_prompt, x_sample, state_conformer_conv, state_pool, state_short_conv, g_pre, g_post,
              w_in, b_gate, ln_v_g, ln_v_b, w_s, b_s, conv_b_w, conv_b_bias, ln_b_g, ln_b_b,
              pool_w, pool_scale, conv_d_w, w_branch, w_out):
    assert x_sample.shape[1] <= CHUNK
    weights = (g_pre, g_post, w_in, b_gate, ln_v_g, ln_v_b, w_s, b_s, conv_b_w, conv_b_bias,
               ln_b_g, ln_b_b, pool_w, pool_scale, conv_d_w, w_branch, w_out)
    bp = x_prompt.shape[0]
    dt = x_prompt.dtype
    zc = jnp.zeros((DEPTH, bp, CONV_K - 1, W_BR), dt)
    zp = jnp.zeros((DEPTH, bp, POOL_STATE, W_BR), dt)
    zs = jnp.zeros((DEPTH, bp, SHORT_K - 1, W_BR), dt)
    y_prompt, conv_p, pool_p, short_p, _ = trunk(x_prompt, zc, zp, zs, 0, *weights)
    y_sample, conv_s, pool_s, short_s, v_s = trunk(x_sample, state_conformer_conv, state_pool,
                                                   state_short_conv, PAST_LEN, *weights)
    return (y_prompt, y_sample, conv_p, pool_p, short_p, conv_s, pool_s, short_s, v_s)
```

```python
import functools

import jax
import jax.numpy as jnp
from jax import lax
from jax.experimental import pallas as pl
from jax.experimental.pallas import tpu as pltpu

D_MODEL = 1024
DEPTH = 2
PAST_LEN = 2048
N_BRANCH = 4
W_BR = D_MODEL // 2
GMLP_CHUNK = 128
GMLP_HEADS = 4
GMLP_HD = W_BR // GMLP_HEADS
CONV_K = 31
POOL_WINDOWS = (2, 4, 8, 16)
POOL_GD = W_BR // len(POOL_WINDOWS)
POOL_STATE = max(POOL_WINDOWS) - 1
SHORT_K = 3
N_PARTS = 12
N_IN = N_PARTS * W_BR + N_BRANCH * D_MODEL
EPS = 1e-6

SUBLANE = 8
HIST_B = 32
HIST_C = 16
HIST_D = 8
CONV_ROWS = 32

VMEM_LIMIT_BYTES = 58 * 1024 * 1024


def _sigmoid(x):
    return 0.5 * jnp.tanh(0.5 * x) + 0.5


def _silu(x):
    return x * _sigmoid(x)


def _dot(a, b):
    return jnp.dot(a, b, preferred_element_type=jnp.float32)


def _layernorm(x, g, b):
    mu = jnp.mean(x, axis=-1, keepdims=True)
    xc = x - mu
    var = jnp.mean(xc * xc, axis=-1, keepdims=True)
    return xc * lax.rsqrt(var + EPS) * g + b


def _rms_scale(x):
    return x * lax.rsqrt(jnp.mean(x * x, axis=-1, keepdims=True) + EPS)


def _causal_depthwise(ext_ref, out_ref, w_ref, hist, taps, tt):
    first = hist - (taps - 1)
    for r0 in range(0, tt, CONV_ROWS):
        rows = min(CONV_ROWS, tt - r0)
        acc = None
        for k in range(taps):
            term = ext_ref[pl.ds(first + r0 + k, rows), :] * w_ref[k:k + 1, :]
            acc = term if acc is None else acc + term
        out_ref[pl.ds(r0, rows), :] = acc


def _layer_kernel(x_ref, conv0_ref, pool0_ref, short0_ref,
                  g_pre_ref, g_post_ref, w_in_ref, b_gate_ref, ln_v_g_ref, ln_v_b_ref,
                  w_s_ref, b_s_ref, conv_b_w_ref, conv_b_bias_ref, ln_b_g_ref, ln_b_b_ref,
                  pool_w_ref, pool_scale_ref, conv_d_w_ref, w_branch_ref, w_out_ref,
                  *refs, tt, ln, start_pos, emit_v):
    if emit_v:
        y_ref, conv_out_ref, pool_out_ref, short_out_ref, v_out_ref = refs[:5]
        ext_b, ext_c, ext_d, tmp = refs[5:]
    else:
        y_ref, conv_out_ref, pool_out_ref, short_out_ref = refs[:4]
        v_out_ref = None
        ext_b, ext_c, ext_d, tmp = refs[4:]

    t_idx = pl.program_id(1)

    @pl.when(t_idx == 0)
    def _load_history():
        ext_b[0:HIST_B - (CONV_K - 1), :] = jnp.zeros((HIST_B - (CONV_K - 1), W_BR), jnp.float32)
        ext_b[HIST_B - (CONV_K - 1):HIST_B, :] = conv0_ref[0]
        ext_c[0:HIST_C - POOL_STATE, :] = jnp.zeros((HIST_C - POOL_STATE, W_BR), jnp.float32)
        ext_c[HIST_C - POOL_STATE:HIST_C, :] = pool0_ref[0]
        ext_d[0:HIST_D - (SHORT_K - 1), :] = jnp.zeros((HIST_D - (SHORT_K - 1), W_BR), jnp.float32)
        ext_d[HIST_D - (SHORT_K - 1):HIST_D, :] = short0_ref[0]

    def part(h, j):
        return _dot(h, w_in_ref[:, j * W_BR:(j + 1) * W_BR])

    def gate(h, j):
        lo = N_PARTS * W_BR + j * D_MODEL
        return _sigmoid(_dot(h, w_in_ref[:, lo:lo + D_MODEL]) + b_gate_ref[:, j * D_MODEL:(j + 1) * D_MODEL])

    x = x_ref[0]
    h = (_rms_scale(x) * g_pre_ref[...]).astype(jnp.bfloat16)

    v_n = _layernorm(part(h, 1), ln_v_g_ref[...], ln_v_b_ref[...])
    if emit_v:
        v_out_ref[0] = v_n
    v_bf = v_n.astype(jnp.bfloat16)
    row = lax.broadcasted_iota(jnp.int32, (GMLP_CHUNK, GMLP_CHUNK), 0)
    col = lax.broadcasted_iota(jnp.int32, (GMLP_CHUNK, GMLP_CHUNK), 1)
    w_tri = [jnp.where(row >= col, w_s_ref[hd], jnp.zeros((), jnp.bfloat16)) for hd in range(GMLP_HEADS)]
    for c0 in range(0, tt, ln):
        for hd in range(GMLP_HEADS):
            v_c = v_bf[c0:c0 + ln, hd * GMLP_HD:(hd + 1) * GMLP_HD]
            if ln < GMLP_CHUNK:
                v_c = jnp.concatenate([v_c, jnp.zeros((GMLP_CHUNK - ln, GMLP_HD), jnp.bfloat16)], axis=0)
            mixed = _dot(w_tri[hd], v_c)[:ln] + b_s_ref[0:ln, hd * GMLP_HD:(hd + 1) * GMLP_HD]
            tmp[c0:c0 + ln, hd * GMLP_HD:(hd + 1) * GMLP_HD] = mixed
    out_a = part(h, 0) * tmp[...] * _silu(part(h, 2))
    merged = gate(h, 0) * _dot(out_a.astype(jnp.bfloat16), w_branch_ref[0])

    ext_b[HIST_B:HIST_B + tt, :] = part(h, 3) * _sigmoid(part(h, 4))
    _causal_depthwise(ext_b, tmp, conv_b_w_ref, HIST_B, CONV_K, tt)
    conv_out_ref[0] = ext_b[tt + HIST_B - (CONV_K - 1):tt + HIST_B, :]
    ext_b[0:HIST_B, :] = ext_b[tt:tt + HIST_B, :]
    cb = tmp[...] + conv_b_bias_ref[...]
    out_b = _silu(_layernorm(cb, ln_b_g_ref[...], ln_b_b_ref[...])) * _silu(part(h, 5))
    merged = merged + gate(h, 1) * _dot(out_b.astype(jnp.bfloat16), w_branch_ref[1])

    ext_c[HIST_C:HIST_C + tt, :] = part(h, 6)
    pos1 = lax.broadcasted_iota(jnp.int32, (tt, POOL_GD), 0) + (start_pos + 1 + t_idx * tt)
    for g, w in enumerate(POOL_WINDOWS):
        sl = slice(g * POOL_GD, (g + 1) * POOL_GD)
        win = ext_c[HIST_C:HIST_C + tt, sl]
        x_tok = win
        for j in range(1, w):
            win = win + ext_c[HIST_C - j:HIST_C - j + tt, sl]
        cnt = jnp.minimum(pos1, w).astype(jnp.float32)
        p = (win / cnt - x_tok).astype(jnp.bfloat16)
        tmp[:, sl] = _dot(p, pool_w_ref[g])
    pool_out_ref[0] = ext_c[tt + HIST_C - POOL_STATE:tt + HIST_C, :]
    ext_c[0:HIST_C, :] = ext_c[tt:tt + HIST_C, :]
    out_c = tmp[...] * pool_scale_ref[...] * _silu(part(h, 7))
    merged = merged + gate(h, 2) * _dot(out_c.astype(jnp.bfloat16), w_branch_ref[2])

    ext_d[HIST_D:HIST_D + tt, :] = part(h, 9) * part(h, 10)
    _causal_depthwise(ext_d, tmp, conv_d_w_ref, HIST_D, SHORT_K, tt)
    short_out_ref[0] = ext_d[tt + HIST_D - (SHORT_K - 1):tt + HIST_D, :]
    ext_d[0:HIST_D, :] = ext_d[tt:tt + HIST_D, :]
    out_d = part(h, 8) * tmp[...] * _silu(part(h, 11))
    merged = merged + gate(h, 3) * _dot(out_d.astype(jnp.bfloat16), w_branch_ref[3])

    y = _dot(merged.astype(jnp.bfloat16), w_out_ref[...])
    y_ref[0] = x + _rms_scale(y) * g_post_ref[...]


def _resident(block_shape, layer):
    zeros = (0,) * (len(block_shape) - 1)
    return pl.BlockSpec(block_shape, lambda b, t: (layer,) + zeros,
                        pipeline_mode=pl.Buffered(1))


def _time_tile(t):
    return min(t, 512)


def _layer(x, conv0, pool0, short0, params, layer, start_pos, emit_v):
    bsz, t, _ = x.shape
    tt = _time_tile(t)
    ln = min(t, GMLP_CHUNK)
    assert t % tt == 0 and tt % ln == 0 and tt % SUBLANE == 0 and tt >= HIST_B

    def per_row(rows):
        return pl.BlockSpec((1, rows, W_BR), lambda b, tq: (b, 0, 0))

    in_specs = [
        pl.BlockSpec((1, tt, D_MODEL), lambda b, tq: (b, tq, 0)),
        per_row(CONV_K - 1), per_row(POOL_STATE), per_row(SHORT_K - 1),
        _resident((None, 1, D_MODEL), layer),
        _resident((None, 1, D_MODEL), layer),
        _resident((None, D_MODEL, N_IN), layer),
        _resident((None, 1, N_BRANCH * D_MODEL), layer),
        _resident((None, 1, W_BR), layer),
        _resident((None, 1, W_BR), layer),
        _resident((None, GMLP_HEADS, GMLP_CHUNK, GMLP_CHUNK), layer),
        _resident((None, GMLP_CHUNK, W_BR), layer),
        _resident((None, CONV_K, W_BR), layer),
        _resident((None, 1, W_BR), layer),
        _resident((None, 1, W_BR), layer),
        _resident((None, 1, W_BR), layer),
        _resident((None, len(POOL_WINDOWS), POOL_GD, POOL_GD), layer),
        _resident((None, 1, W_BR), layer),
        _resident((None, SHORT_K, W_BR), layer),
        _resident((None, N_BRANCH, W_BR, D_MODEL), layer),
        _resident((None, D_MODEL, D_MODEL), layer),
    ]
    out_shape = [
        jax.ShapeDtypeStruct((bsz, t, D_MODEL), jnp.float32),
        jax.ShapeDtypeStruct((bsz, CONV_K - 1, W_BR), jnp.float32),
        jax.ShapeDtypeStruct((bsz, POOL_STATE, W_BR), jnp.float32),
        jax.ShapeDtypeStruct((bsz, SHORT_K - 1, W_BR), jnp.float32),
    ]
    out_specs = [
        pl.BlockSpec((1, tt, D_MODEL), lambda b, tq: (b, tq, 0)),
        per_row(CONV_K - 1), per_row(POOL_STATE), per_row(SHORT_K - 1),
    ]
    if emit_v:
        out_shape.append(jax.ShapeDtypeStruct((bsz, t, W_BR), jnp.float32))
        out_specs.append(pl.BlockSpec((1, tt, W_BR), lambda b, tq: (b, tq, 0)))

    body = functools.partial(_layer_kernel, tt=tt, ln=ln, start_pos=start_pos, emit_v=emit_v)
    return pl.pallas_call(
        body,
        grid=(bsz, t // tt),
        in_specs=in_specs,
        out_specs=out_specs,
        out_shape=out_shape,
        scratch_shapes=[
            pltpu.VMEM((HIST_B + tt, W_BR), jnp.float32),
            pltpu.VMEM((HIST_C + tt, W_BR), jnp.float32),
            pltpu.VMEM((HIST_D + tt, W_BR), jnp.float32),
            pltpu.VMEM((tt, W_BR), jnp.float32),
        ],
        compiler_params=pltpu.CompilerParams(
            dimension_semantics=("arbitrary", "arbitrary"),
            vmem_limit_bytes=VMEM_LIMIT_BYTES),
        name=f"layer{layer}_{'sample' if emit_v else 'prompt'}",
    )(x, conv0, pool0, short0, *params)


def _trunk(x, conv_states, pool_states, short_states, start_pos, params, emit_v):
    new_conv, new_pool, new_short, v_rows = [], [], [], []
    for layer in range(DEPTH):
        outs = _layer(x, conv_states[layer], pool_states[layer], short_states[layer],
                      params, layer, start_pos, emit_v)
        x = outs[0]
        new_conv.append(outs[1])
        new_pool.append(outs[2])
        new_short.append(outs[3])
        if emit_v:
            v_rows.append(outs[4])
    v = jnp.stack(v_rows) if emit_v else None
    return x, jnp.stack(new_conv), jnp.stack(new_pool), jnp.stack(new_short), v


def kernel(x_prompt, x_sample, state_conformer_conv, state_pool, state_short_conv, g_pre, g_post,
           w_in, b_gate, ln_v_g, ln_v_b, w_s, b_s, conv_b_w, conv_b_bias, ln_b_g, ln_b_b,
           pool_w, pool_scale, conv_d_w, w_branch, w_out):
    bf16 = jnp.bfloat16

    def row_vec(p):
        return p[:, None, :]

    def params_for(t):
        ln = min(t, GMLP_CHUNK)
        pad = GMLP_CHUNK - ln
        w_s_sq = jnp.pad(w_s[:, :, :ln, :ln], ((0, 0), (0, 0), (0, pad), (0, pad)))
        b_s_rows = jnp.repeat(jnp.swapaxes(b_s, 1, 2), GMLP_HD, axis=2)
        return (row_vec(g_pre), row_vec(g_post), w_in.astype(bf16), row_vec(b_gate),
                row_vec(ln_v_g), row_vec(ln_v_b), w_s_sq.astype(bf16), b_s_rows,
                conv_b_w, row_vec(conv_b_bias), row_vec(ln_b_g), row_vec(ln_b_b),
                pool_w.astype(bf16), row_vec(pool_scale), conv_d_w,
                w_branch.astype(bf16), w_out.astype(bf16))

    bp = x_prompt.shape[0]
    dt = x_prompt.dtype
    zc = jnp.zeros((DEPTH, bp, CONV_K - 1, W_BR), dt)
    zp = jnp.zeros((DEPTH, bp, POOL_STATE, W_BR), dt)
    zs = jnp.zeros((DEPTH, bp, SHORT_K - 1, W_BR), dt)
    y_prompt, conv_p, pool_p, short_p, _ = _trunk(
        x_prompt, zc, zp, zs, 0, params_for(x_prompt.shape[1]), False)
    y_sample, conv_s, pool_s, short_s, v_s = _trunk(
        x_sample, state_conformer_conv, state_pool, state_short_conv, PAST_LEN,
        params_for(x_sample.shape[1]), True)
    return (y_prompt, y_sample, conv_p, pool_p, short_p, conv_s, pool_s, short_s, v_s)
```
